```python
import math
import jax, jax.numpy as jnp
from jax import lax
import numpy as np

D_MODEL = 1024
BATCH = 4
SEQ = 4096
DEPTH = 2

GRID_W = 64
CTX_LEN = 256
N_MIXERS = 4
D_MIX = D_MODEL
GROUP_W = D_MIX // N_MIXERS
M_HEADS = 4
M_HD = GROUP_W // M_HEADS
M_CHUNK = 64
M_GATES = 2 * 2 * M_HEADS
A_HEADS = 4
A_HD = GROUP_W // (2 * A_HEADS)
N_FREQ = A_HD // 4
ROPE_THETA = 10000.0
Q_BLOCK = 128
POOL_WINDOWS = (2, 4, 8, 16)
POOL_GW = GROUP_W // len(POOL_WINDOWS)
CONV_K = 31
FF_RAW = -(-8 * D_MODEL // 3)
D_FF = -(-FF_RAW // 256) * 256
IN_SPLITS = (GROUP_W, GROUP_W, GROUP_W, GROUP_W, M_GATES, GROUP_W, GROUP_W, GROUP_W, GROUP_W, 2 * GROUP_W)
IN_COLS = sum(IN_SPLITS)

kernel_name = "hybrid_parallel_group_flow_block"


def rms_norm(x, g, eps=1e-6):
    xf = x.astype(jnp.float32)
    y = xf * lax.rsqrt(jnp.mean(xf * xf, axis=-1, keepdims=True) + eps)
    return (y * g).astype(x.dtype)


def layer_norm(x, g, b, eps=1e-5):
    xf = x.astype(jnp.float32)
    mu = jnp.mean(xf, axis=-1, keepdims=True)
    var = jnp.mean(jnp.square(xf - mu), axis=-1, keepdims=True)
    return ((xf - mu) * lax.rsqrt(var + eps) * g + b).astype(x.dtype)


def modulate(x, shift, scale):
    return x * (1 + scale) + shift


def split_columns(p):
    idx, acc = [], 0
    for s in IN_SPLITS[:-1]:
        acc += s
        idx.append(acc)
    return jnp.split(p, idx, axis=-1)


def flip(a):
    return jnp.flip(a, axis=1)


def axial_rope_tables(n_tokens):
    n_rows = n_tokens // GRID_W
    rows = jnp.repeat(jnp.arange(n_rows), GRID_W).astype(jnp.float32)
    cols = jnp.tile(jnp.arange(GRID_W), n_rows).astype(jnp.float32)
    freqs = ROPE_THETA ** (-jnp.arange(N_FREQ, dtype=jnp.float32) / N_FREQ)
    ang = jnp.stack([rows[:, None] * freqs, cols[:, None] * freqs], axis=1)
    return jnp.cos(ang), jnp.sin(ang)


def apply_axial_rope(x, cos, sin):
    shp = x.shape
    xr = x.reshape(shp[:-1] + (2, 2, N_FREQ))
    x0, x1 = xr[..., 0, :], xr[..., 1, :]
    c = cos[None, :, None, None].astype(x.dtype)
    s = sin[None, :, None, None].astype(x.dtype)
    out = jnp.stack([x0 * c - x1 * s, x1 * c + x0 * s], axis=-2)
    return out.reshape(shp)


def mlstm_inputs(mq, mk, mv, mg, gate_b):
    B, L, _ = mq.shape
    q = mq.reshape(B, L, M_HEADS, M_HD).astype(jnp.float32)
    k = mk.reshape(B, L, M_HEADS, M_HD).astype(jnp.float32) * (M_HD ** -0.5)
    v = mv.reshape(B, L, M_HEADS, M_HD).astype(jnp.float32)
    gates = mg.reshape(B, L, 2, 2, M_HEADS).astype(jnp.float32) + gate_b
    log_i = gates[:, :, :, 0]
    log_f = jax.nn.log_sigmoid(gates[:, :, :, 1])
    return q, k, v, log_i, log_f


def mlstm_scan(q, k, v, log_i, log_f, state):
    B, L, H, d = q.shape
    nc = L // M_CHUNK

    def to_chunks(a):
        return jnp.moveaxis(a.reshape((B, nc, M_CHUNK) + a.shape[2:]), 1, 0)

    causal = jnp.tril(jnp.ones((M_CHUNK, M_CHUNK), dtype=bool))[None, :, :, None]

    def step(carry, inp):
        C, n, m = carry
        qc, kc, vc, ic, fc = inp
        b = jnp.cumsum(fc, axis=1)
        b_tot = b[:, -1]
        inter = b + m[:, None]
        log_d = b[:, :, None, :] - b[:, None, :, :] + ic[:, None, :, :]
        log_d = jnp.where(causal, log_d, -jnp.inf)
        m_t = jnp.maximum(inter, jnp.max(log_d, axis=2))
        sw = jnp.exp(log_d - m_t[:, :, None, :]) * jnp.einsum('bthd,bshd->btsh', qc, kc)
        e_inter = jnp.exp(inter - m_t)
        num = e_inter[..., None] * jnp.einsum('bhvk,bthk->bthv', C, qc) + jnp.einsum('btsh,bshv->bthv', sw, vc)
        den = e_inter * jnp.einsum('bhk,bthk->bth', n, qc) + jnp.sum(sw, axis=2)
        h = num / jnp.maximum(jnp.abs(den), jnp.exp(-m_t))[..., None]
        g = b_tot[:, None] - b + ic
        m_new = jnp.maximum(b_tot + m, jnp.max(g, axis=1))
        e_old = jnp.exp(b_tot + m - m_new)
        e_s = jnp.exp(g - m_new[:, None])
        C_new = e_old[..., None, None] * C + jnp.einsum('bsh,bshv,bshk->bhvk', e_s, vc, kc)
        n_new = e_old[..., None] * n + jnp.einsum('bsh,bshk->bhk', e_s, kc)
        return (C_new, n_new, m_new), h

    final, hs = lax.scan(step, state, (to_chunks(q), to_chunks(k), to_chunks(v), to_chunks(log_i), to_chunks(log_f)))
    return jnp.moveaxis(hs, 0, 1).reshape(B, L, H, d), final


def mlstm_out(h, o_pre, norm_g):
    B, L = o_pre.shape[:2]
    o = jax.nn.sigmoid(o_pre.reshape(B, L, M_HEADS, M_HD).astype(jnp.float32))
    return (o * rms_norm(h, norm_g)).reshape(B, L, GROUP_W).astype(o_pre.dtype)


def attn_inputs(aq, ak, av, q_norm_g, k_norm_g, rope):
    B, L, _ = aq.shape
    q = rms_norm(aq.reshape(B, L, A_HEADS, 2, A_HD), q_norm_g)
    k = rms_norm(ak.reshape(B, L, A_HEADS, 2, A_HD), k_norm_g)
    v = av.reshape(B, L, A_HEADS, 2 * A_HD)
    if rope is not None:
        q = apply_axial_rope(q, rope[0], rope[1])
        k = apply_axial_rope(k, rope[0], rope[1])
    return q, k, v


def diff_softmax_attend(q, k, v, lam):
    s = jnp.einsum('bqhmd,bkhmd->bhmqk', q, k).astype(jnp.float32) * (A_HD ** -0.5)
    p = jax.nn.softmax(s, axis=-1)
    a = p[:, :, 0] - lam * p[:, :, 1]
    return jnp.einsum('bhqk,bkhe->bqhe', a.astype(v.dtype), v)


def blocked_latent_attention(q, k_all, v_all, lam):
    B, L = q.shape[:2]
    nb = L // Q_BLOCK
    qb = jnp.moveaxis(q.reshape((B, nb, Q_BLOCK) + q.shape[2:]), 1, 0)
    ob = lax.map(lambda qq: diff_softmax_attend(qq, k_all, v_all, lam), qb)
    return jnp.moveaxis(ob, 0, 1).reshape((B, L) + ob.shape[3:])


def attn_out(o, subln_g, lam_init):
    B, L = o.shape[:2]
    return (rms_norm(o, subln_g) * (1 - lam_init)).reshape(B, L, GROUP_W)


def multiscale_pool(u):
    B, L, _ = u.shape
    uf = u.astype(jnp.float32)
    cs = jnp.concatenate([jnp.zeros((B, 1, GROUP_W), jnp.float32), jnp.cumsum(uf, axis=1)], axis=1)
    t = jnp.arange(L)
    outs = []
    for gi, w in enumerate(POOL_WINDOWS):
        lo = jnp.clip(t - w // 2, 0, L - 1)
        hi = jnp.clip(t + (w - 1 - w // 2), 0, L - 1)
        csg = cs[:, :, gi * POOL_GW:(gi + 1) * POOL_GW]
        mean = (csg[:, hi + 1] - csg[:, lo]) / (hi - lo + 1).astype(jnp.float32)[None, :, None]
        outs.append(mean - uf[:, :, gi * POOL_GW:(gi + 1) * POOL_GW])
    return jnp.concatenate(outs, axis=-1).astype(u.dtype)


def pool_branch(u, pool_w, pool_scale):
    B, L, _ = u.shape
    y = multiscale_pool(u).reshape(B, L, len(POOL_WINDOWS), POOL_GW)
    y = jnp.einsum('blgc,gcd->blgd', y, pool_w).reshape(B, L, GROUP_W)
    return y * pool_scale


def conformer_conv(u, dw_w, dw_b, ln_g, ln_b, pw_w):
    a, g = jnp.split(u, 2, axis=-1)
    y = a * jax.nn.sigmoid(g)
    y = lax.conv_general_dilated(y, dw_w[:, None, :], window_strides=(1,),
                                 padding=((CONV_K // 2, CONV_K // 2),),
                                 dimension_numbers=('NWC', 'WIO', 'NWC'),
                                 feature_group_count=GROUP_W) + dw_b
    y = jax.nn.silu(layer_norm(y, ln_g, ln_b))
    return y @ pw_w


def swiglu(u, w_in, w_out):
    g, up = jnp.split(u @ w_in, 2, axis=-1)
    return (jax.nn.silu(g) * up) @ w_out


def hybrid_mixer(u_lat, u_ctx, rope, w_in, gate_b, m_norm_g, q_norm_g, k_norm_g, lam_q, lam_k, subln_g,
                 pool_w, pool_scale, dw_w, dw_b, ln_g, ln_b, pw_w, lam_init, need_ctx):
    pl = split_columns(u_lat @ w_in)
    pc = split_columns(u_ctx @ w_in)
    B = u_lat.shape[0]
    ql, kl, vl, il, fl = mlstm_inputs(pl[0], pl[1], pl[2], pl[4], gate_b)
    qc, kc, vc, ic, fc = mlstm_inputs(pc[0], pc[1], pc[2], pc[4], gate_b)
    zero = (jnp.zeros((B, M_HEADS, M_HD, M_HD), jnp.float32),
            jnp.zeros((B, M_HEADS, M_HD), jnp.float32),
            jnp.zeros((B, M_HEADS), jnp.float32))
    hcf, st_f = mlstm_scan(qc, kc, vc, ic[:, :, 0], fc[:, :, 0], zero)
    hcb, st_b = mlstm_scan(flip(qc), flip(kc), flip(vc), flip(ic[:, :, 1]), flip(fc[:, :, 1]), zero)
    hlf, _ = mlstm_scan(ql, kl, vl, il[:, :, 0], fl[:, :, 0], st_f)
    hlb, _ = mlstm_scan(flip(ql), flip(kl), flip(vl), flip(il[:, :, 1]), flip(fl[:, :, 1]), st_b)
    m_lat = mlstm_out(hlf + flip(hlb), pl[3], m_norm_g)
    lam = (jnp.exp(jnp.sum(lam_q[0] * lam_k[0])) - jnp.exp(jnp.sum(lam_q[1] * lam_k[1])) + lam_init).astype(jnp.float32)
    aql, akl, avl = attn_inputs(pl[5], pl[6], pl[7], q_norm_g, k_norm_g, rope)
    aqc, akc, avc = attn_inputs(pc[5], pc[6], pc[7], q_norm_g, k_norm_g, None)
    k_all = jnp.concatenate([akc, akl], axis=1)
    v_all = jnp.concatenate([avc, avl], axis=1)
    a_lat = attn_out(blocked_latent_attention(aql, k_all, v_all, lam), subln_g, lam_init)
    p_lat = pool_branch(pl[8], pool_w, pool_scale)
    c_lat = conformer_conv(pl[9], dw_w, dw_b, ln_g, ln_b, pw_w)
    y_lat = jnp.concatenate([m_lat, a_lat, p_lat, c_lat], axis=-1)
    if not need_ctx:
        return y_lat, None
    m_ctx = mlstm_out(hcf + flip(hcb), pc[3], m_norm_g)
    a_ctx = attn_out(diff_softmax_attend(aqc, akc, avc, lam), subln_g, lam_init)
    p_ctx = pool_branch(pc[8], pool_w, pool_scale)
    c_ctx_out = conformer_conv(pc[9], dw_w, dw_b, ln_g, ln_b, pw_w)
    y_ctx = jnp.concatenate([m_ctx, a_ctx, p_ctx, c_ctx_out], axis=-1)
    return y_lat, y_ctx


def setup_inputs(seed: int = 0) -> dict:
    key = jax.random.key(seed)
    ks = iter(jax.random.split(key, 32))
    nrm = lambda shape, s: jax.random.normal(next(ks), shape, jnp.float32) * s
    gate_b = jnp.stack([nrm((DEPTH, 2, M_HEADS), 0.1),
                        3.0 + 3.0 * jax.random.uniform(next(ks), (DEPTH, 2, M_HEADS), jnp.float32)], axis=2)
    return {
        "x": nrm((BATCH, SEQ, D_MODEL), 1.0),
        "c": nrm((BATCH, D_MODEL), 1.0),
        "ctx": nrm((BATCH, CTX_LEN, D_MODEL), 1.0),
        "c_ctx": nrm((D_MODEL,), 1.0),
        "norm1_g": 1.0 + nrm((DEPTH, D_MODEL), 0.02),
        "norm2_g": 1.0 + nrm((DEPTH, D_MODEL), 0.02),
        "mod_w": nrm((DEPTH, D_MODEL, 6 * D_MODEL), D_MODEL ** -0.5),
        "mod_b": nrm((DEPTH, 6 * D_MODEL), 0.02),
        "w_in": nrm((DEPTH, D_MODEL, IN_COLS), D_MODEL ** -0.5),
        "mlstm_gate_b": gate_b,
        "mlstm_norm_g": 1.0 + nrm((DEPTH, M_HD), 0.02),
        "attn_q_norm_g": 1.0 + nrm((DEPTH, A_HD), 0.02),
        "attn_k_norm_g": 1.0 + nrm((DEPTH, A_HD), 0.02),
        "lambda_q": nrm((DEPTH, 2, A_HD), 0.1),
        "lambda_k": nrm((DEPTH, 2, A_HD), 0.1),
        "attn_subln_g": 1.0 + nrm((DEPTH, 2 * A_HD), 0.02),
        "pool_w": nrm((DEPTH, len(POOL_WINDOWS), POOL_GW, POOL_GW), POOL_GW ** -0.5),
        "pool_scale": 1.0 + nrm((DEPTH, GROUP_W), 0.1),
        "conv_dw_w": nrm((DEPTH, CONV_K, GROUP_W), CONV_K ** -0.5),
        "conv_dw_b": nrm((DEPTH, GROUP_W), 0.02),
        "conv_ln_g": 1.0 + nrm((DEPTH, GROUP_W), 0.02),
        "conv_ln_b": nrm((DEPTH, GROUP_W), 0.02),
        "conv_pw_w": nrm((DEPTH, GROUP_W, GROUP_W), GROUP_W ** -0.5),
        "w_out": nrm((DEPTH, D_MIX, D_MODEL), D_MIX ** -0.5),
        "ffn_w_in": nrm((DEPTH, D_MODEL, 2 * D_FF), D_MODEL ** -0.5),
        "ffn_w_out": nrm((DEPTH, D_FF, D_MODEL), D_FF ** -0.5),
    }


def reference(x, c, ctx, c_ctx, norm1_g, norm2_g, mod_w, mod_b, w_in, mlstm_gate_b, mlstm_norm_g,
              attn_q_norm_g, attn_k_norm_g, lambda_q, lambda_k, attn_subln_g, pool_w, pool_scale,
              conv_dw_w, conv_dw_b, conv_ln_g, conv_ln_b, conv_pw_w, w_out, ffn_w_in, ffn_w_out):
    L = x.shape[1]
    rope = axial_rope_tables(L)
    h, hc = x, ctx
    s_c = jax.nn.silu(c)
    s_cc = jax.nn.silu(c_ctx)
    for l in range(DEPTH):
        need_ctx = l < DEPTH - 1
        lam_init = 0.8 - 0.6 * math.exp(-0.3 * l)
        mod_lat = (s_c @ mod_w[l] + mod_b[l])[:, None, :]
        mod_ctx = (s_cc @ mod_w[l] + mod_b[l])[None, None, :]
        sh1, sc1, g1, sh2, sc2, g2 = jnp.split(mod_lat, 6, axis=-1)
        csh1, csc1, cg1, csh2, csc2, cg2 = jnp.split(mod_ctx, 6, axis=-1)
        u_lat = modulate(rms_norm(h, norm1_g[l]), sh1, sc1)
        u_ctx = modulate(rms_norm(hc, norm1_g[l]), csh1, csc1)
        y_lat, y_ctx = hybrid_mixer(u_lat, u_ctx, rope, w_in[l], mlstm_gate_b[l], mlstm_norm_g[l],
                                    attn_q_norm_g[l], attn_k_norm_g[l], lambda_q[l], lambda_k[l],
                                    attn_subln_g[l], pool_w[l], pool_scale[l], conv_dw_w[l], conv_dw_b[l],
                                    conv_ln_g[l], conv_ln_b[l], conv_pw_w[l], lam_init, need_ctx)
        h = h + g1 * (y_lat @ w_out[l])
        h = h + g2 * swiglu(modulate(rms_norm(h, norm2_g[l]), sh2, sc2), ffn_w_in[l], ffn_w_out[l])
        if need_ctx:
            hc = hc + cg1 * (y_ctx @ w_out[l])
            hc = hc + cg2 * swiglu(modulate(rms_norm(hc, norm2_g[l]), csh2, csc2), ffn_w_in[l], ffn_w_out[l])
    return h
```

```python
import functools
import math

import jax
import jax.numpy as jnp
from jax import lax
from jax.experimental import pallas as pl
from jax.experimental.pallas import tpu as pltpu

F32 = jnp.float32
BF16 = jnp.bfloat16

GROUP_W = 256
M_HEADS = 4
M_HD = 64
A_HEADS = 4
A_HD = 32
N_FREQ = A_HD // 4
GRID_W = 64
ROPE_THETA = 10000.0
POOL_WINDOWS = (2, 4, 8, 16)
CONV_K = 31
RMS_EPS = 1e-6
LN_EPS = 1e-5

ROW_TILE = 512
M_CHUNK = 256
Q_TILE = 256
KEY_CHUNK = 512
LOCAL_TILE = 256
HALO = 16
FFN_CHUNK = 256
V7X_VMEM_LIMIT = 56 * 1024 * 1024

IN_COLS_R = 11 * 256


def _dot(a, b):
    return jnp.dot(a, b, preferred_element_type=F32)


def _split3(x):
    hi = x.astype(BF16)
    r1 = x - hi.astype(F32)
    mid = r1.astype(BF16)
    lo = (r1 - mid.astype(F32)).astype(BF16)
    return hi, mid, lo


def _block_ones(n, group):
    sh = int(math.log2(group))
    r = lax.broadcasted_iota(jnp.int32, (n, n), 0) >> sh
    c = lax.broadcasted_iota(jnp.int32, (n, n), 1) >> sh
    return jnp.where(r == c, 1.0, 0.0).astype(BF16)


def _group_sumsq(x, group):
    sq = x * x
    hi = sq.astype(BF16)
    lo = (sq - hi.astype(F32)).astype(BF16)
    bd = _block_ones(x.shape[-1], group)
    return _dot(hi, bd) + _dot(lo, bd)


def _sigmoid(x):
    return 1.0 / (1.0 + jnp.exp(-x))


def _const_spec(shape):
    nd = len(shape)
    return pl.BlockSpec(shape, lambda *_: (0,) * nd, pipeline_mode=pl.Buffered(1))


def _mod_kernel(c_ref, w_ref, b_ref, o_ref):
    cv = c_ref[...]
    s = (cv * _sigmoid(cv)).astype(BF16)
    o_ref[...] = _dot(s, w_ref[...].astype(BF16)) + b_ref[...]


def _mod_call(cc, mod_w, mod_b):
    depth, d, n = mod_w.shape
    nb = 512
    return pl.pallas_call(
        _mod_kernel,
        out_shape=jax.ShapeDtypeStruct((depth, 8, n), F32),
        grid=(depth, n // nb),
        in_specs=[
            pl.BlockSpec((8, d), lambda l, j: (0, 0)),
            pl.BlockSpec((None, d, nb), lambda l, j: (l, 0, j)),
            pl.BlockSpec((None, 1, nb), lambda l, j: (l, 0, j)),
        ],
        out_specs=pl.BlockSpec((None, 8, nb), lambda l, j: (l, 0, j)),
        name="mod_vectors",
    )(cc, mod_w, mod_b.reshape(depth, 1, n))


def _inproj_kernel(*refs, rope):
    if rope:
        (h_ref, mod_ref, g1_ref, w_ref, gq_ref, gk_ref, cos_ref, sin_ref,
         mq_ref, mkt_ref, mv_ref, mo_ref, gt_ref, aq_ref, akt_ref, av_ref, pu_ref, glu_ref) = refs
    else:
        (h_ref, mod_ref, g1_ref, w_ref, gq_ref, gk_ref,
         mq_ref, mkt_ref, mv_ref, mo_ref, gt_ref, aq_ref, akt_ref, av_ref, pu_ref, glu_ref) = refs
    x = h_ref[...]
    ms = jnp.mean(x * x, axis=-1, keepdims=True)
    y = x * lax.rsqrt(ms + RMS_EPS) * g1_ref[...]
    u = (y * (1.0 + mod_ref[1:2, :]) + mod_ref[0:1, :]).astype(BF16)

    def sec(i, n=256):
        return _dot(u, w_ref[:, i * 256:i * 256 + n])

    mq_ref[...] = sec(0).astype(BF16)
    mkt_ref[...] = sec(1).T.astype(BF16)
    mv_ref[...] = sec(2).astype(BF16)
    mo_ref[...] = sec(3).astype(BF16)
    gt_ref[...] = sec(4)

    def qk(i, g_ref):
        t = sec(i)
        t = t * lax.rsqrt(_group_sumsq(t, A_HD) * (1.0 / A_HD) + RMS_EPS) * g_ref[...]
        if rope:
            lane = lax.broadcasted_iota(jnp.int32, t.shape, 1)
            n = t.shape[1]
            partner = jnp.where((lane & (2 * N_FREQ - 1)) < N_FREQ,
                                pltpu.roll(t, n - N_FREQ, 1), pltpu.roll(t, N_FREQ, 1))
            t = t * cos_ref[...] + partner * sin_ref[...]
        return t

    aq_ref[...] = qk(5, gq_ref).astype(BF16)
    akt_ref[...] = qk(6, gk_ref).T.astype(BF16)
    av_ref[...] = sec(7).astype(BF16)
    pu_ref[...] = sec(8)
    glu_ref[...] = sec(9) * _sigmoid(sec(10))


def _inproj_call(h, mod4, layer, mod_row_fn, g1, w_r, gq, gk, rope_tabs):
    rows, d = h.shape
    tm = ROW_TILE
    nt = rows // tm
    rope = rope_tabs is not None
    in_specs = [
        pl.BlockSpec((tm, d), lambda i: (i, 0)),
        pl.BlockSpec((None, None, 6, d), lambda i: (layer, mod_row_fn(i), 0, 0)),
        _const_spec((1, d)),
        _const_spec((d, IN_COLS_R)),
        _const_spec((1, 256)),
        _const_spec((1, 256)),
    ]
    args = [h, mod4, g1, w_r, gq, gk]
    if rope:
        nper = rope_tabs[0].shape[0] // tm
        in_specs += [pl.BlockSpec((tm, 256), lambda i: (i % nper, 0))] * 2
        args += list(rope_tabs)
    row_spec = pl.BlockSpec((tm, 256), lambda i: (i, 0))
    col_spec = pl.BlockSpec((256, tm), lambda i: (0, i))
    rb = jax.ShapeDtypeStruct((rows, 256), BF16)
    rf = jax.ShapeDtypeStruct((rows, 256), F32)
    cb = jax.ShapeDtypeStruct((256, rows), BF16)
    out_shape = [rb, cb, rb, rb, rf, rb, cb, rb, rf, rf]
    out_specs = [row_spec, col_spec, row_spec, row_spec, row_spec, row_spec, col_spec, row_spec, row_spec, row_spec]
    return pl.pallas_call(
        functools.partial(_inproj_kernel, rope=rope),
        out_shape=out_shape,
        grid=(nt,),
        in_specs=in_specs,
        out_specs=out_specs,
        compiler_params=pltpu.CompilerParams(dimension_semantics=("arbitrary",), vmem_limit_bytes=V7X_VMEM_LIMIT),
        name="in_proj_rope" if rope else "in_proj_ctx",
    )(*args)


def _mlstm_kernel(*refs, n_ctx, n_lat, write_ctx):
    (qc_ref, ktc_ref, vc_ref, oc_ref, gc_ref, ql_ref, ktl_ref, vl_ref, ol_ref, gl_ref, gb_ref, ng_ref) = refs[:12]
    if write_ctx:
        out_l_ref, out_c_ref, hf_ref, hb_ref, st_ref, m_ref = refs[12:]
    else:
        out_l_ref, hf_ref, hb_ref, st_ref, m_ref = refs[12:]
        out_c_ref = None
    C = M_CHUNK
    lc = n_ctx * C

    st_ref[...] = jnp.zeros(st_ref.shape, F32)
    m_ref[...] = jnp.zeros(m_ref.shape, F32)

    row = lax.broadcasted_iota(jnp.int32, (C, C), 0)
    col = lax.broadcasted_iota(jnp.int32, (C, C), 1)
    lane128 = lax.broadcasted_iota(jnp.int32, (C, 128), 1)
    sub128 = lax.broadcasted_iota(jnp.int32, (128, C), 0)
    ones_blk = jnp.ones((C, 128), BF16)

    def chunk(d, q, kt, v, g, h_ref, row0):
        keep = (col <= row) if d == 0 else (col >= row)
        tri = jnp.where(keep, 1.0, 0.0).astype(BF16)
        gg = g + gb_ref[...]
        li = gg[:, :128]
        gf = gg[:, 128:]
        lf = jnp.minimum(gf, 0.0) - jnp.log1p(jnp.exp(-jnp.abs(gf)))
        f_hi, f_mid, f_lo = _split3(lf)
        b = _dot(tri, f_hi) + _dot(tri, f_mid) + _dot(tri, f_lo)
        btot = b[C - 1:C, :] if d == 0 else b[0:1, :]
        m_row = m_ref[d:d + 1, :]
        gm = btot - b + li
        m_new = jnp.maximum(btot + m_row, jnp.max(gm, axis=0, keepdims=True))
        e_s = jnp.exp(gm - m_new)
        e_old = jnp.exp(btot + m_row - m_new)
        inter = b + m_row
        r_t = (li - b).T
        pairs = []
        for p in range(M_HEADS // 2):
            q_p = q[:, 128 * p:128 * p + 128]
            kt_p = kt[128 * p:128 * p + 128, :]
            rhs = jnp.concatenate([v[:, 128 * p:128 * p + 128], ones_blk], axis=1)
            rhs32 = rhs.astype(F32)
            halves = []
            for hh in range(2):
                j = d * M_HEADS + 2 * p + hh
                bc = b[:, j:j + 1]
                logd = jnp.where(keep, bc + r_t[j:j + 1, :], -jnp.inf)
                m_t = jnp.maximum(inter[:, j:j + 1], jnp.max(logd, axis=1, keepdims=True))
                dm = jnp.exp(logd - m_t)
                qm = jnp.where((lane128 >> 6) == hh, q_p, jnp.zeros_like(q_p))
                sw = (dm * _dot(qm, kt_p)).astype(BF16)
                st = st_ref[j]
                nd = jnp.exp(inter[:, j:j + 1] - m_t) * _dot(qm, st.astype(BF16)) + _dot(sw, rhs)
                den = jnp.maximum(jnp.abs(nd[:, 128:]), jnp.exp(-m_t))
                halves.append(nd[:, :128] / den)
                ev = (e_s[:, j:j + 1] * rhs32).astype(BF16)
                ktm = jnp.where((sub128 >> 6) == hh, kt_p, jnp.zeros_like(kt_p))
                st_ref[j] = e_old[:, j:j + 1] * st + _dot(ktm, ev)
            pairs.append(jnp.where(lane128 < M_HD, halves[0], halves[1]))
        h_ref[pl.ds(row0, C), :] = jnp.concatenate(pairs, axis=1)
        m_ref[d:d + 1, :] = m_new

    for c in range(n_ctx):
        r0 = c * C
        chunk(0, qc_ref[r0:r0 + C, :], ktc_ref[:, r0:r0 + C], vc_ref[r0:r0 + C, :], gc_ref[r0:r0 + C, :], hf_ref, r0)
        rb = (n_ctx - 1 - c) * C
        chunk(1, qc_ref[rb:rb + C, :], ktc_ref[:, rb:rb + C], vc_ref[rb:rb + C, :], gc_ref[rb:rb + C, :], hb_ref, rb)

    def lat_step(s, carry):
        rf = pl.multiple_of(s * C, C)
        chunk(0, ql_ref[pl.ds(rf, C), :], ktl_ref[:, pl.ds(rf, C)], vl_ref[pl.ds(rf, C), :], gl_ref[pl.ds(rf, C), :],
              hf_ref, lc + rf)
        rb = pl.multiple_of((n_lat - 1 - s) * C, C)
        chunk(1, ql_ref[pl.ds(rb, C), :], ktl_ref[:, pl.ds(rb, C)], vl_ref[pl.ds(rb, C), :], gl_ref[pl.ds(rb, C), :],
              hb_ref, lc + rb)
        return carry

    lax.fori_loop(0, n_lat, lat_step, 0)

    def finish(hrow, o_pre):
        hs = hf_ref[pl.ds(hrow, C), :] + hb_ref[pl.ds(hrow, C), :]
        yn = hs * lax.rsqrt(_group_sumsq(hs, M_HD) * (1.0 / M_HD) + RMS_EPS) * ng_ref[...]
        return (_sigmoid(o_pre.astype(F32)) * yn).astype(BF16)

    if write_ctx:
        for c in range(n_ctx):
            r0 = c * C
            out_c_ref[r0:r0 + C, :] = finish(r0, oc_ref[r0:r0 + C, :])

    def fin_step(s, carry):
        r0 = pl.multiple_of(s * C, C)
        out_l_ref[pl.ds(r0, C), :] = finish(lc + r0, ol_ref[pl.ds(r0, C), :])
        return carry

    lax.fori_loop(0, n_lat, fin_step, 0)


def _mlstm_call(ctx_p, lat_p, gb, ng, batch, write_ctx):
    mq_c, mkt_c, mv_c, mo_c, gt_c = ctx_p
    mq_l, mkt_l, mv_l, mo_l, gt_l = lat_p
    lc = mq_c.shape[0] // batch
    ll = mq_l.shape[0] // batch
    n_ctx, n_lat = lc // M_CHUNK, ll // M_CHUNK

    def rows(n):
        return pl.BlockSpec((n, 256), lambda b: (b, 0))

    def cols(n):
        return pl.BlockSpec((256, n), lambda b: (0, b))

    in_specs = [rows(lc), cols(lc), rows(lc), rows(lc), rows(lc),
                rows(ll), cols(ll), rows(ll), rows(ll), rows(ll),
                _const_spec((1, 256)), _const_spec((1, 256))]
    out_shape = [jax.ShapeDtypeStruct((batch * ll, 256), BF16)]
    out_specs = [rows(ll)]
    if write_ctx:
        out_shape.append(jax.ShapeDtypeStruct((batch * lc, 256), BF16))
        out_specs.append(rows(lc))
    res = pl.pallas_call(
        functools.partial(_mlstm_kernel, n_ctx=n_ctx, n_lat=n_lat, write_ctx=write_ctx),
        out_shape=out_shape,
        grid=(batch,),
        in_specs=in_specs,
        out_specs=out_specs,
        scratch_shapes=[
            pltpu.VMEM((lc + ll, 256), F32),
            pltpu.VMEM((lc + ll, 256), F32),
            pltpu.VMEM((2 * M_HEADS, 128, 256), F32),
            pltpu.VMEM((8, 128), F32),
        ],
        compiler_params=pltpu.CompilerParams(dimension_semantics=("arbitrary",), vmem_limit_bytes=V7X_VMEM_LIMIT),
        name="mlstm",
    )(mq_c, mkt_c, mv_c, mo_c, gt_c, mq_l, mkt_l, mv_l, mo_l, gt_l, gb, ng)
    return (res[0], res[1]) if write_ctx else (res[0], None)


def _attn_kernel(*refs, key_lens, lam_init):
    n_src = len(key_lens)
    q_ref = refs[0]
    kv_refs = refs[1:1 + 2 * n_src]
    lq_ref, lk_ref, sg_ref, out_ref, s_ref = refs[1 + 2 * n_src:]
    tq = q_ref.shape[0]
    lam = (jnp.exp(jnp.sum(lq_ref[0:1, :] * lk_ref[0:1, :], axis=1, keepdims=True))
           - jnp.exp(jnp.sum(lq_ref[1:2, :] * lk_ref[1:2, :], axis=1, keepdims=True)) + lam_init)
    q = q_ref[...]
    lane = lax.broadcasted_iota(jnp.int32, (tq, 128), 1)
    ones_blk = jnp.ones((KEY_CHUNK, 128), BF16)

    chunks, off = [], 0
    for si, n in enumerate(key_lens):
        for k0 in range(0, n, KEY_CHUNK):
            kn = min(KEY_CHUNK, n - k0)
            chunks.append((si, k0, kn, off))
            off += kn

    for p in range(A_HEADS // 2):
        q_p = q[:, 128 * p:128 * p + 128]
        halves = []
        for hh in range(2):
            normed = []
            for m in range(2):
                g = 2 * hh + m
                buf = m
                qm = jnp.where((lane >> 5) == g, q_p, jnp.zeros_like(q_p))
                for (si, k0, kn, o) in chunks:
                    kt_ref = kv_refs[2 * si]
                    s_ref[buf, :, o:o + kn] = _dot(qm, kt_ref[128 * p:128 * p + 128, k0:k0 + kn])
                mx = jnp.max(s_ref[buf], axis=1, keepdims=True)
                acc = jnp.zeros((tq, 256), F32)
                for (si, k0, kn, o) in chunks:
                    v_ref = kv_refs[2 * si + 1]
                    e = jnp.exp2(s_ref[buf, :, o:o + kn] - mx).astype(BF16)
                    rhs = jnp.concatenate([v_ref[k0:k0 + kn, 128 * p:128 * p + 128], ones_blk[:kn]], axis=1)
                    acc = acc + _dot(e, rhs)
                normed.append(acc[:, :128] / acc[:, 128:])
            halves.append(normed[0] - lam * normed[1])
        pair = jnp.where(lane < 2 * A_HD, halves[0], halves[1])
        yn = pair * lax.rsqrt(_group_sumsq(pair, 2 * A_HD) * (1.0 / (2 * A_HD)) + RMS_EPS) * sg_ref[...]
        out_ref[:, 128 * p:128 * p + 128] = (yn * (1.0 - lam_init)).astype(BF16)


def _attn_call(aq, srcs, lam_q, lam_k, sg, batch, lam_init):
    lq = aq.shape[0] // batch
    tq = min(Q_TILE, lq)
    nq = lq // tq
    key_lens = tuple(v.shape[0] // batch for _, v in srcs)
    in_specs = [pl.BlockSpec((tq, 256), lambda b, i: (b * nq + i, 0))]
    args = [aq]
    for (kt, v), n in zip(srcs, key_lens):
        in_specs.append(pl.BlockSpec((256, n), lambda b, i: (0, b)))
        in_specs.append(pl.BlockSpec((n, 256), lambda b, i: (b, 0)))
        args += [kt, v]
    in_specs += [_const_spec((2, A_HD)), _const_spec((2, A_HD)), _const_spec((1, 128))]
    args += [lam_q, lam_k, sg]
    return pl.pallas_call(
        functools.partial(_attn_kernel, key_lens=key_lens, lam_init=lam_init),
        out_shape=jax.ShapeDtypeStruct((batch * lq, 256), BF16),
        grid=(batch, nq),
        in_specs=in_specs,
        out_specs=pl.BlockSpec((tq, 256), lambda b, i: (b * nq + i, 0)),
        scratch_shapes=[pltpu.VMEM((2, tq, sum(key_lens)), F32)],
        compiler_params=pltpu.CompilerParams(dimension_semantics=("arbitrary", "arbitrary"),
                                             vmem_limit_bytes=V7X_VMEM_LIMIT),
        name="diff_attn",
    )(*args)


def _local_kernel(pu_ref, gl_ref, dww_ref, dwb_ref, lng_ref, lnb_ref, pww_ref, plw_ref, pls_ref,
                  p_out, c_out, ppad, gpad, psh, gsh, *, seq_len):
    T = LOCAL_TILE
    t = pl.program_id(1)

    @pl.when(t == 0)
    def _():
        z = jnp.zeros((HALO, GROUP_W), F32)
        for pad, src in ((ppad, pu_ref), (gpad, gl_ref)):
            pad[0:HALO, :] = z
            pad[HALO:HALO + seq_len, :] = src[...]
            pad[HALO + seq_len:2 * HALO + seq_len, :] = z

    base = pl.multiple_of(t * T, T)
    win = T + 2 * HALO
    for pad, sh in ((ppad, psh), (gpad, gsh)):
        xx = pad[pl.ds(base, win), :]
        sh[0] = xx[:T + 24]
        for r in range(1, 8):
            sh[r] = pltpu.roll(xx, win - r, 0)[:T + 24]

    def tap(sh, pos):
        a, r = divmod(pos, 8)
        return sh[r, 8 * a:8 * a + T, :]

    lane = lax.broadcasted_iota(jnp.int32, (T, GROUP_W), 1)
    grp = lane >> 6

    acc = tap(psh, HALO - 1) + tap(psh, HALO)
    for o in (-2, 1, -4, -3, 2, 3, -8, -7, -6, -5, 4, 5, 6, 7):
        gmin = 1 if o in (-2, 1) else (2 if o in (-4, -3, 2, 3) else 3)
        acc = acc + jnp.where(grp >= gmin, tap(psh, HALO + o), 0.0)
    tg = t * T + lax.broadcasted_iota(jnp.int32, (T, GROUP_W), 0)
    half = jnp.left_shift(1, grp)
    lo = jnp.maximum(tg - half, 0)
    hi = jnp.minimum(tg + half - 1, seq_len - 1)
    cnt = (hi - lo + 1).astype(F32)
    pooled = acc / cnt - tap(psh, HALO)
    p_out[...] = (_dot(pooled.astype(BF16), plw_ref[...]) * pls_ref[...]).astype(BF16)

    y = tap(gsh, 1) * dww_ref[0:1, :]
    for k in range(1, CONV_K):
        y = y + tap(gsh, k + 1) * dww_ref[k:k + 1, :]
    y = y + dwb_ref[...]
    mu = jnp.mean(y, axis=-1, keepdims=True)
    dlt = y - mu
    var = jnp.mean(dlt * dlt, axis=-1, keepdims=True)
    yn = dlt * lax.rsqrt(var + LN_EPS) * lng_ref[...] + lnb_ref[...]
    act = yn * _sigmoid(yn)
    c_out[...] = _dot(act.astype(BF16), pww_ref[...]).astype(BF16)


def _local_call(pu, glu, dww, dwb, lng, lnb, pww, plw, pls, batch):
    seq_len = pu.shape[0] // batch
    T = LOCAL_TILE
    nt = seq_len // T
    seq_spec = pl.BlockSpec((seq_len, 256), lambda b, t: (b, 0))
    out_spec = pl.BlockSpec((T, 256), lambda b, t: (b * nt + t, 0))
    ob = jax.ShapeDtypeStruct((batch * seq_len, 256), BF16)
    return pl.pallas_call(
        functools.partial(_local_kernel, seq_len=seq_len),
        out_shape=[ob, ob],
        grid=(batch, nt),
        in_specs=[seq_spec, seq_spec, _const_spec((CONV_K, 256)), _const_spec((1, 256)), _const_spec((1, 256)),
                  _const_spec((1, 256)), _const_spec((256, 256)), _const_spec((256, 256)), _const_spec((1, 256))],
        out_specs=[out_spec, out_spec],
        scratch_shapes=[
            pltpu.VMEM((seq_len + 2 * HALO, 256), F32),
            pltpu.VMEM((seq_len + 2 * HALO, 256), F32),
            pltpu.VMEM((8, T + 24, 256), F32),
            pltpu.VMEM((8, T + 24, 256), F32),
        ],
        compiler_params=pltpu.CompilerParams(dimension_semantics=("arbitrary", "arbitrary"),
                                             vmem_limit_bytes=V7X_VMEM_LIMIT),
        name="pool_conv",
    )(pu, glu, dww, dwb, lng, lnb, pww, plw, pls)


def _outffn_kernel(h_ref, m_ref, a_ref, p_ref, c_ref, mod_ref, g2_ref, wo_ref, wi_ref, wf_ref, out_ref, *, d_ff):
    y = (_dot(m_ref[...], wo_ref[0:256, :]) + _dot(a_ref[...], wo_ref[256:512, :])
         + _dot(p_ref[...], wo_ref[512:768, :]) + _dot(c_ref[...], wo_ref[768:1024, :]))
    h1 = h_ref[...] + mod_ref[2:3, :] * y
    ms = jnp.mean(h1 * h1, axis=-1, keepdims=True)
    u = (h1 * lax.rsqrt(ms + RMS_EPS) * g2_ref[...] * (1.0 + mod_ref[4:5, :]) + mod_ref[3:4, :]).astype(BF16)
    acc = jnp.zeros(h1.shape, F32)
    for j in range(d_ff // FFN_CHUNK):
        c0 = j * FFN_CHUNK
        g = _dot(u, wi_ref[:, c0:c0 + FFN_CHUNK])
        up = _dot(u, wi_ref[:, d_ff + c0:d_ff + c0 + FFN_CHUNK])
        act = (g * _sigmoid(g) * up).astype(BF16)
        acc = acc + _dot(act, wf_ref[c0:c0 + FFN_CHUNK, :])
    out_ref[...] = h1 + mod_ref[5:6, :] * acc


def _outffn_call(h, ym, ya, yp, yc, mod4, layer, mod_row_fn, g2, wo, wi, wf):
    rows, d = h.shape
    tm = ROW_TILE
    d_ff = wf.shape[0]
    mix = pl.BlockSpec((tm, 256), lambda i: (i, 0))
    return pl.pallas_call(
        functools.partial(_outffn_kernel, d_ff=d_ff),
        out_shape=jax.ShapeDtypeStruct((rows, d), F32),
        grid=(rows // tm,),
        in_specs=[
            pl.BlockSpec((tm, d), lambda i: (i, 0)), mix, mix, mix, mix,
            pl.BlockSpec((None, None, 6, d), lambda i: (layer, mod_row_fn(i), 0, 0)),
            _const_spec((1, d)), _const_spec((d, d)), _const_spec((d, 2 * d_ff)), _const_spec((d_ff, d)),
        ],
        out_specs=pl.BlockSpec((tm, d), lambda i: (i, 0)),
        compiler_params=pltpu.CompilerParams(dimension_semantics=("arbitrary",), vmem_limit_bytes=V7X_VMEM_LIMIT),
        name="out_proj_ffn",
    )(h, ym, ya, yp, yc, mod4, g2, wo, wi, wf)


def _rope_tables(n_tokens):
    n_rows = n_tokens // GRID_W
    rows = jnp.repeat(jnp.arange(n_rows), GRID_W).astype(F32)
    cols = jnp.tile(jnp.arange(GRID_W), n_rows).astype(F32)
    freqs = ROPE_THETA ** (-jnp.arange(N_FREQ, dtype=F32) / N_FREQ)
    ar, ac = rows[:, None] * freqs, cols[:, None] * freqs
    cos32 = jnp.concatenate([jnp.cos(ar), jnp.cos(ar), jnp.cos(ac), jnp.cos(ac)], axis=1)
    sin32 = jnp.concatenate([-jnp.sin(ar), jnp.sin(ar), -jnp.sin(ac), jnp.sin(ac)], axis=1)
    reps = GROUP_W // A_HD
    return jnp.tile(cos32, (1, reps)), jnp.tile(sin32, (1, reps))


def _pad_lanes(a, n):
    return jnp.pad(a, ((0, 0), (0, n - a.shape[1])))


def _prep_w_in(w):
    mq, mk, mv, mo = (w[:, i * 256:(i + 1) * 256] for i in range(4))
    mg = w[:, 1024:1040].reshape(-1, 2, 2, M_HEADS)
    rest = w[:, 1040:]
    gi = _pad_lanes(mg[:, :, 0, :].reshape(-1, 2 * M_HEADS), 128)
    gf = _pad_lanes(mg[:, :, 1, :].reshape(-1, 2 * M_HEADS), 128)
    return jnp.concatenate([mq, mk * (M_HD ** -0.5), mv, mo, gi, gf, rest], axis=1).astype(BF16)


def kernel(x, c, ctx, c_ctx, norm1_g, norm2_g, mod_w, mod_b, w_in, mlstm_gate_b, mlstm_norm_g, attn_q_norm_g,
           attn_k_norm_g, lambda_q, lambda_k, attn_subln_g, pool_w, pool_scale, conv_dw_w, conv_dw_b, conv_ln_g,
           conv_ln_b, conv_pw_w, w_out, ffn_w_in, ffn_w_out):
    batch, seq, d = x.shape
    lc = ctx.shape[1]
    depth = w_in.shape[0]
    assert seq % ROW_TILE == 0 and (batch * lc) % ROW_TILE == 0 and lc % M_CHUNK == 0 and seq % M_CHUNK == 0
    tiles_per_seq = seq // ROW_TILE

    h_lat = x.reshape(batch * seq, d)
    h_ctx = ctx.reshape(batch * lc, d)
    cc = jnp.concatenate([c, c_ctx[None, :], jnp.zeros((8 - batch - 1, d), F32)], axis=0)
    mod4 = _mod_call(cc, mod_w, mod_b).reshape(depth, 8, 6, d)
    rope_tabs = _rope_tables(seq)
    lat_row = lambda i: i // tiles_per_seq
    ctx_row = lambda i: batch

    for l in range(depth):
        need_ctx = l < depth - 1
        lam_init = 0.8 - 0.6 * math.exp(-0.3 * l)
        w_r = _prep_w_in(w_in[l])
        g1 = norm1_g[l][None, :]
        g2 = norm2_g[l][None, :]
        gq = jnp.tile(attn_q_norm_g[l] * (A_HD ** -0.5 * math.log2(math.e)), GROUP_W // A_HD)[None, :]
        gk = jnp.tile(attn_k_norm_g[l], GROUP_W // A_HD)[None, :]
        gb = mlstm_gate_b[l]
        gb = jnp.concatenate([_pad_lanes(gb[:, 0, :].reshape(1, -1), 128), _pad_lanes(gb[:, 1, :].reshape(1, -1), 128)], axis=1)
        ng = jnp.tile(mlstm_norm_g[l], M_HEADS)[None, :]
        sg = jnp.tile(attn_subln_g[l], 2)[None, :]
        plw = jax.scipy.linalg.block_diag(*[pool_w[l, g] for g in range(len(POOL_WINDOWS))]).astype(BF16)
        pls = pool_scale[l][None, :]
        local_w = (conv_dw_w[l], conv_dw_b[l][None, :], conv_ln_g[l][None, :], conv_ln_b[l][None, :],
                   conv_pw_w[l].astype(BF16), plw, pls)
        wo = w_out[l].astype(BF16)
        wi = ffn_w_in[l].astype(BF16)
        wf = ffn_w_out[l].astype(BF16)

        lat = _inproj_call(h_lat, mod4, l, lat_row, g1, w_r, gq, gk, rope_tabs)
        cx = _inproj_call(h_ctx, mod4, l, ctx_row, g1, w_r, gq, gk, None)
        m_lat, m_ctx = _mlstm_call(cx[0:5], lat[0:5], gb, ng, batch, need_ctx)
        a_lat = _attn_call(lat[5], [(cx[6], cx[7]), (lat[6], lat[7])], lambda_q[l], lambda_k[l], sg, batch, lam_init)
        p_lat, c_lat = _local_call(lat[8], lat[9], *local_w, batch)
        h_lat = _outffn_call(h_lat, m_lat, a_lat, p_lat, c_lat, mod4, l, lat_row, g2, wo, wi, wf)
        if need_ctx:
            a_ctx = _attn_call(cx[5], [(cx[6], cx[7])], lambda_q[l], lambda_k[l], sg, batch, lam_init)
            p_ctx, c_ctx_o = _local_call(cx[8], cx[9], *local_w, batch)
            h_ctx = _outffn_call(h_ctx, m_ctx, a_ctx, p_ctx, c_ctx_o, mod4, l, ctx_row, g2, wo, wi, wf)
    return h_lat.reshape(batch, seq, d)
```

```python
import functools
import math

import jax
import jax.numpy as jnp
from jax import lax
from jax.experimental import pallas as pl
from jax.experimental.pallas import tpu as pltpu

F32 = jnp.float32
BF16 = jnp.bfloat16

GROUP_W = 256
M_HEADS = 4
M_HD = 64
A_HEADS = 4
A_HD = 32
N_FREQ = A_HD // 4
GRID_W = 64
ROPE_THETA = 10000.0
POOL_WINDOWS = (2, 4, 8, 16)
CONV_K = 31
RMS_EPS = 1e-6
LN_EPS = 1e-5

ROW_TILE = 512
M_CHUNK = 256
Q_TILE = 256
KEY_CHUNK = 512
SCORE_EXP2_LIMIT = 40.0
LOCAL_TILE = 256
HALO = 16
FFN_CHUNK = 256
V7X_VMEM_LIMIT = 56 * 1024 * 1024

IN_COLS_R = 11 * 256


def _dot(a, b):
    return jnp.dot(a, b, preferred_element_type=F32)


def _split3(x):
    hi = x.astype(BF16)
    r1 = x - hi.astype(F32)
    mid = r1.astype(BF16)
    lo = (r1 - mid.astype(F32)).astype(BF16)
    return hi, mid, lo


def _block_ones(n, group):
    sh = int(math.log2(group))
    r = lax.broadcasted_iota(jnp.int32, (n, n), 0) >> sh
    c = lax.broadcasted_iota(jnp.int32, (n, n), 1) >> sh
    return jnp.where(r == c, 1.0, 0.0).astype(BF16)


def _group_sumsq(x, group):
    sq = x * x
    hi = sq.astype(BF16)
    lo = (sq - hi.astype(F32)).astype(BF16)
    bd = _block_ones(x.shape[-1], group)
    return _dot(hi, bd) + _dot(lo, bd)


def _sigmoid(x):
    return 1.0 / (1.0 + jnp.exp(-x))


def _const_spec(shape):
    nd = len(shape)
    return pl.BlockSpec(shape, lambda *_: (0,) * nd, pipeline_mode=pl.Buffered(1))


def _mod_kernel(c_ref, w_ref, b_ref, o_ref):
    cv = c_ref[...]
    s = (cv * _sigmoid(cv)).astype(BF16)
    o_ref[...] = _dot(s, w_ref[...].astype(BF16)) + b_ref[...]


def _mod_call(cc, mod_w, mod_b):
    depth, d, n = mod_w.shape
    nb = 512
    return pl.pallas_call(
        _mod_kernel,
        out_shape=jax.ShapeDtypeStruct((depth, 8, n), F32),
        grid=(depth, n // nb),
        in_specs=[
            pl.BlockSpec((8, d), lambda l, j: (0, 0)),
            pl.BlockSpec((None, d, nb), lambda l, j: (l, 0, j)),
            pl.BlockSpec((None, 1, nb), lambda l, j: (l, 0, j)),
        ],
        out_specs=pl.BlockSpec((None, 8, nb), lambda l, j: (l, 0, j)),
        name="mod_vectors",
    )(cc, mod_w, mod_b.reshape(depth, 1, n))


def _inproj_kernel(*refs, rope):
    if rope:
        (h_ref, mod_ref, g1_ref, w_ref, gq_ref, gk_ref, cos_ref, sin_ref,
         mq_ref, mkt_ref, mv_ref, mo_ref, gt_ref, aq_ref, akt_ref, av_ref, pu_ref, glu_ref) = refs
    else:
        (h_ref, mod_ref, g1_ref, w_ref, gq_ref, gk_ref,
         mq_ref, mkt_ref, mv_ref, mo_ref, gt_ref, aq_ref, akt_ref, av_ref, pu_ref, glu_ref) = refs
    x = h_ref[...]
    ms = jnp.mean(x * x, axis=-1, keepdims=True)
    y = x * lax.rsqrt(ms + RMS_EPS) * g1_ref[...]
    u = (y * (1.0 + mod_ref[1:2, :]) + mod_ref[0:1, :]).astype(BF16)

    def sec(i, n=256):
        return _dot(u, w_ref[:, i * 256:i * 256 + n])

    mq_ref[...] = sec(0).astype(BF16)
    mkt_ref[...] = sec(1).T.astype(BF16)
    mv_ref[...] = sec(2).astype(BF16)
    mo_ref[...] = sec(3).astype(BF16)
    gt_ref[...] = sec(4)

    def qk(i, g_ref):
        t = sec(i)
        t = t * lax.rsqrt(_group_sumsq(t, A_HD) * (1.0 / A_HD) + RMS_EPS) * g_ref[...]
        if rope:
            lane = lax.broadcasted_iota(jnp.int32, t.shape, 1)
            n = t.shape[1]
            partner = jnp.where((lane & (2 * N_FREQ - 1)) < N_FREQ,
                                pltpu.roll(t, n - N_FREQ, 1), pltpu.roll(t, N_FREQ, 1))
            t = t * cos_ref[...] + partner * sin_ref[...]
        return t

    aq_ref[...] = qk(5, gq_ref).astype(BF16)
    akt_ref[...] = qk(6, gk_ref).T.astype(BF16)
    av_ref[...] = sec(7).astype(BF16)
    pu_ref[...] = sec(8)
    glu_ref[...] = sec(9) * _sigmoid(sec(10))


def _inproj_call(h, mod4, layer, mod_row_fn, g1, w_r, gq, gk, rope_tabs):
    rows, d = h.shape
    tm = ROW_TILE
    nt = rows // tm
    rope = rope_tabs is not None
    in_specs = [
        pl.BlockSpec((tm, d), lambda i: (i, 0)),
        pl.BlockSpec((None, None, 6, d), lambda i: (layer, mod_row_fn(i), 0, 0)),
        _const_spec((1, d)),
        _const_spec((d, IN_COLS_R)),
        _const_spec((1, 256)),
        _const_spec((1, 256)),
    ]
    args = [h, mod4, g1, w_r, gq, gk]
    if rope:
        nper = rope_tabs[0].shape[0] // tm
        in_specs += [pl.BlockSpec((tm, 256), lambda i: (i % nper, 0))] * 2
        args += list(rope_tabs)
    row_spec = pl.BlockSpec((tm, 256), lambda i: (i, 0))
    col_spec = pl.BlockSpec((256, tm), lambda i: (0, i))
    rb = jax.ShapeDtypeStruct((rows, 256), BF16)
    rf = jax.ShapeDtypeStruct((rows, 256), F32)
    cb = jax.ShapeDtypeStruct((256, rows), BF16)
    out_shape = [rb, cb, rb, rb, rf, rb, cb, rb, rf, rf]
    out_specs = [row_spec, col_spec, row_spec, row_spec, row_spec, row_spec, col_spec, row_spec, row_spec, row_spec]
    return pl.pallas_call(
        functools.partial(_inproj_kernel, rope=rope),
        out_shape=out_shape,
        grid=(nt,),
        in_specs=in_specs,
        out_specs=out_specs,
        compiler_params=pltpu.CompilerParams(dimension_semantics=("arbitrary",), vmem_limit_bytes=V7X_VMEM_LIMIT),
        name="in_proj_rope" if rope else "in_proj_ctx",
    )(*args)


def _mlstm_kernel(*refs, n_ctx, n_lat, write_ctx):
    (qc_ref, ktc_ref, vc_ref, oc_ref, gc_ref, ql_ref, ktl_ref, vl_ref, ol_ref, gl_ref, gb_ref, ng_ref) = refs[:12]
    if write_ctx:
        out_l_ref, out_c_ref, hf_ref, hb_ref, st_ref, m_ref = refs[12:]
    else:
        out_l_ref, hf_ref, hb_ref, st_ref, m_ref = refs[12:]
        out_c_ref = None
    C = M_CHUNK
    lc = n_ctx * C

    st_ref[...] = jnp.zeros(st_ref.shape, F32)
    m_ref[...] = jnp.zeros(m_ref.shape, F32)

    row = lax.broadcasted_iota(jnp.int32, (C, C), 0)
    col = lax.broadcasted_iota(jnp.int32, (C, C), 1)
    lane128 = lax.broadcasted_iota(jnp.int32, (C, 128), 1)
    sub128 = lax.broadcasted_iota(jnp.int32, (128, C), 0)
    ones_blk = jnp.ones((C, 128), BF16)

    def chunk(d, q, kt, v, g, h_ref, row0):
        keep = (col <= row) if d == 0 else (col >= row)
        tri = jnp.where(keep, 1.0, 0.0).astype(BF16)
        gg = g + gb_ref[...]
        li = gg[:, :128]
        gf = gg[:, 128:]
        lf = jnp.minimum(gf, 0.0) - jnp.log1p(jnp.exp(-jnp.abs(gf)))
        f_hi, f_mid, f_lo = _split3(lf)
        b = _dot(tri, f_hi) + _dot(tri, f_mid) + _dot(tri, f_lo)
        btot = b[C - 1:C, :] if d == 0 else b[0:1, :]
        m_row = m_ref[d:d + 1, :]
        gm = btot - b + li
        m_new = jnp.maximum(btot + m_row, jnp.max(gm, axis=0, keepdims=True))
        e_s = jnp.exp(gm - m_new)
        e_old = jnp.exp(btot + m_row - m_new)
        r = li - b
        r_t = r.T
        es_t = e_s.T
        pairs = []
        for p in range(M_HEADS // 2):
            q_p = q[:, 128 * p:128 * p + 128]
            kt_p = kt[128 * p:128 * p + 128, :]
            kt_p32 = kt_p.astype(F32)
            rhs = jnp.concatenate([v[:, 128 * p:128 * p + 128], ones_blk], axis=1)
            halves = []
            for hh in range(2):
                j = d * M_HEADS + 2 * p + hh
                masked = jnp.where(keep, r_t[j:j + 1, :], -jnp.inf)
                m_prev = m_row[:, j:j + 1]
                big_m = jnp.maximum(jnp.max(masked, axis=1, keepdims=True), m_prev)
                dm = jnp.exp(masked - big_m)
                qm = jnp.where((lane128 >> 6) == hh, q_p, jnp.zeros_like(q_p))
                sw = (dm * _dot(qm, kt_p)).astype(BF16)
                st = st_ref[j]
                nd = jnp.exp(m_prev - big_m) * _dot(qm, st.astype(BF16)) + _dot(sw, rhs)
                den = jnp.maximum(jnp.abs(nd[:, 128:]), jnp.exp(-(b[:, j:j + 1] + big_m)))
                halves.append(nd[:, :128] / den)
                ktm = (jnp.where((sub128 >> 6) == hh, kt_p32, 0.0) * es_t[j:j + 1, :]).astype(BF16)
                st_ref[j] = e_old[:, j:j + 1] * st + _dot(ktm, rhs)
            pairs.append(jnp.where(lane128 < M_HD, halves[0], halves[1]))
        h_ref[pl.ds(row0, C), :] = jnp.concatenate(pairs, axis=1)
        m_ref[d:d + 1, :] = m_new

    for c in range(n_ctx):
        r0 = c * C
        chunk(0, qc_ref[r0:r0 + C, :], ktc_ref[:, r0:r0 + C], vc_ref[r0:r0 + C, :], gc_ref[r0:r0 + C, :], hf_ref, r0)
        rb = (n_ctx - 1 - c) * C
        chunk(1, qc_ref[rb:rb + C, :], ktc_ref[:, rb:rb + C], vc_ref[rb:rb + C, :], gc_ref[rb:rb + C, :], hb_ref, rb)

    def lat_step(s, carry):
        rf = pl.multiple_of(s * C, C)
        chunk(0, ql_ref[pl.ds(rf, C), :], ktl_ref[:, pl.ds(rf, C)], vl_ref[pl.ds(rf, C), :], gl_ref[pl.ds(rf, C), :],
              hf_ref, lc + rf)
        rb = pl.multiple_of((n_lat - 1 - s) * C, C)
        chunk(1, ql_ref[pl.ds(rb, C), :], ktl_ref[:, pl.ds(rb, C)], vl_ref[pl.ds(rb, C), :], gl_ref[pl.ds(rb, C), :],
              hb_ref, lc + rb)
        return carry

    lax.fori_loop(0, n_lat, lat_step, 0)

    def finish(hrow, o_pre):
        hs = hf_ref[pl.ds(hrow, C), :] + hb_ref[pl.ds(hrow, C), :]
        yn = hs * lax.rsqrt(_group_sumsq(hs, M_HD) * (1.0 / M_HD) + RMS_EPS) * ng_ref[...]
        return (_sigmoid(o_pre.astype(F32)) * yn).astype(BF16)

    if write_ctx:
        for c in range(n_ctx):
            r0 = c * C
            out_c_ref[r0:r0 + C, :] = finish(r0, oc_ref[r0:r0 + C, :])

    def fin_step(s, carry):
        r0 = pl.multiple_of(s * C, C)
        out_l_ref[pl.ds(r0, C), :] = finish(lc + r0, ol_ref[pl.ds(r0, C), :])
        return carry

    lax.fori_loop(0, n_lat, fin_step, 0)


def _mlstm_call(ctx_p, lat_p, gb, ng, batch, write_ctx):
    mq_c, mkt_c, mv_c, mo_c, gt_c = ctx_p
    mq_l, mkt_l, mv_l, mo_l, gt_l = lat_p
    lc = mq_c.shape[0] // batch
    ll = mq_l.shape[0] // batch
    n_ctx, n_lat = lc // M_CHUNK, ll // M_CHUNK

    def rows(n):
        return pl.BlockSpec((n, 256), lambda b: (b, 0))

    def cols(n):
        return pl.BlockSpec((256, n), lambda b: (0, b))

    in_specs = [rows(lc), cols(lc), rows(lc), rows(lc), rows(lc),
                rows(ll), cols(ll), rows(ll), rows(ll), rows(ll),
                _const_spec((1, 256)), _const_spec((1, 256))]
    out_shape = [jax.ShapeDtypeStruct((batch * ll, 256), BF16)]
    out_specs = [rows(ll)]
    if write_ctx:
        out_shape.append(jax.ShapeDtypeStruct((batch * lc, 256), BF16))
        out_specs.append(rows(lc))
    res = pl.pallas_call(
        functools.partial(_mlstm_kernel, n_ctx=n_ctx, n_lat=n_lat, write_ctx=write_ctx),
        out_shape=out_shape,
        grid=(batch,),
        in_specs=in_specs,
        out_specs=out_specs,
        scratch_shapes=[
            pltpu.VMEM((lc + ll, 256), F32),
            pltpu.VMEM((lc + ll, 256), F32),
            pltpu.VMEM((2 * M_HEADS, 128, 256), F32),
            pltpu.VMEM((8, 128), F32),
        ],
        compiler_params=pltpu.CompilerParams(dimension_semantics=("arbitrary",), vmem_limit_bytes=V7X_VMEM_LIMIT),
        name="mlstm",
    )(mq_c, mkt_c, mv_c, mo_c, gt_c, mq_l, mkt_l, mv_l, mo_l, gt_l, gb, ng)
    return (res[0], res[1]) if write_ctx else (res[0], None)


def _attn_kernel(*refs, key_lens, lam_init):
    n_src = len(key_lens)
    q_ref = refs[0]
    kv_refs = refs[1:1 + 2 * n_src]
    lq_ref, lk_ref, sg_ref, out_ref, s_ref = refs[1 + 2 * n_src:]
    tq = q_ref.shape[0]
    lam = (jnp.exp(jnp.sum(lq_ref[0:1, :] * lk_ref[0:1, :], axis=1, keepdims=True))
           - jnp.exp(jnp.sum(lq_ref[1:2, :] * lk_ref[1:2, :], axis=1, keepdims=True)) + lam_init)
    q = q_ref[...]
    lane = lax.broadcasted_iota(jnp.int32, (tq, 128), 1)
    ones_blk = jnp.ones((KEY_CHUNK, 128), BF16)

    chunks, off = [], 0
    for si, n in enumerate(key_lens):
        for k0 in range(0, n, KEY_CHUNK):
            kn = min(KEY_CHUNK, n - k0)
            chunks.append((si, k0, kn, off))
            off += kn

    for p in range(A_HEADS // 2):
        q_p = q[:, 128 * p:128 * p + 128]
        halves = []
        for hh in range(2):
            normed = []
            for m in range(2):
                g = 2 * hh + m
                buf = m
                qm = jnp.where((lane >> 5) == g, q_p, jnp.zeros_like(q_p))
                for (si, k0, kn, o) in chunks:
                    kt_ref = kv_refs[2 * si]
                    s_ref[buf, :, o:o + kn] = _dot(qm, kt_ref[128 * p:128 * p + 128, k0:k0 + kn])
                mx = jnp.max(s_ref[buf], axis=1, keepdims=True)
                acc = jnp.zeros((tq, 256), F32)
                for (si, k0, kn, o) in chunks:
                    v_ref = kv_refs[2 * si + 1]
                    e = jnp.exp2(s_ref[buf, :, o:o + kn] - mx).astype(BF16)
                    rhs = jnp.concatenate([v_ref[k0:k0 + kn, 128 * p:128 * p + 128], ones_blk[:kn]], axis=1)
                    acc = acc + _dot(e, rhs)
                normed.append(acc[:, :128] / acc[:, 128:])
            halves.append(normed[0] - lam * normed[1])
        pair = jnp.where(lane < 2 * A_HD, halves[0], halves[1])
        yn = pair * lax.rsqrt(_group_sumsq(pair, 2 * A_HD) * (1.0 / (2 * A_HD)) + RMS_EPS) * sg_ref[...]
        out_ref[:, 128 * p:128 * p + 128] = (yn * (1.0 - lam_init)).astype(BF16)


def _attn_bounded_kernel(*refs, key_lens, lam_init):
    n_src = len(key_lens)
    q_ref = refs[0]
    kv_refs = refs[1:1 + 2 * n_src]
    lq_ref, lk_ref, sg_ref, out_ref, e_ref = refs[1 + 2 * n_src:]
    tq = q_ref.shape[0]
    lam = (jnp.exp(jnp.sum(lq_ref[0:1, :] * lk_ref[0:1, :], axis=1, keepdims=True))
           - jnp.exp(jnp.sum(lq_ref[1:2, :] * lk_ref[1:2, :], axis=1, keepdims=True)) + lam_init)
    q = q_ref[...]
    lane = lax.broadcasted_iota(jnp.int32, (tq, 128), 1)

    chunks, off = [], 0
    for si, n in enumerate(key_lens):
        for k0 in range(0, n, KEY_CHUNK):
            kn = min(KEY_CHUNK, n - k0)
            chunks.append((si, k0, kn, off))
            off += kn

    for p in range(A_HEADS // 2):
        q_p = q[:, 128 * p:128 * p + 128]
        halves = []
        for hh in range(2):
            inv, part_sums = [], []
            for m in range(2):
                g = 2 * hh + m
                qm = jnp.where((lane >> 5) == g, q_p, jnp.zeros_like(q_p))
                part = jnp.zeros((tq, 128), F32)
                for (si, k0, kn, o) in chunks:
                    kt_ref = kv_refs[2 * si]
                    e = jnp.exp2(_dot(qm, kt_ref[128 * p:128 * p + 128, k0:k0 + kn]))
                    for c0 in range(0, kn, 128):
                        part = part + e[:, c0:c0 + 128]
                    e_ref[hh, m, :, o:o + kn] = e.astype(BF16)
                part_sums.append(jnp.sum(part, axis=1, keepdims=True))
                inv.append(1.0 / part_sums[m])
            ratio = (lam * inv[1] * part_sums[0]).astype(BF16)
            acc = jnp.zeros((tq, 128), F32)
            for (si, k0, kn, o) in chunks:
                v_ref = kv_refs[2 * si + 1]
                a = e_ref[hh, 0, :, o:o + kn] - e_ref[hh, 1, :, o:o + kn] * ratio
                acc = acc + _dot(a, v_ref[k0:k0 + kn, 128 * p:128 * p + 128])
            halves.append(acc * inv[0])
        pair = jnp.where(lane < 2 * A_HD, halves[0], halves[1])
        yn = pair * lax.rsqrt(_group_sumsq(pair, 2 * A_HD) * (1.0 / (2 * A_HD)) + RMS_EPS) * sg_ref[...]
        out_ref[:, 128 * p:128 * p + 128] = (yn * (1.0 - lam_init)).astype(BF16)


def _attn_call(aq, srcs, lam_q, lam_k, sg, batch, lam_init, bounded=False):
    lq = aq.shape[0] // batch
    tq = min(Q_TILE, lq)
    nq = lq // tq
    key_lens = tuple(v.shape[0] // batch for _, v in srcs)
    if bounded:
        body = functools.partial(_attn_bounded_kernel, key_lens=key_lens, lam_init=lam_init)
        scratch = pltpu.VMEM((2, 2, tq, sum(key_lens)), BF16)
    else:
        body = functools.partial(_attn_kernel, key_lens=key_lens, lam_init=lam_init)
        scratch = pltpu.VMEM((2, tq, sum(key_lens)), F32)
    in_specs = [pl.BlockSpec((tq, 256), lambda b, i: (b * nq + i, 0))]
    args = [aq]
    for (kt, v), n in zip(srcs, key_lens):
        in_specs.append(pl.BlockSpec((256, n), lambda b, i: (0, b)))
        in_specs.append(pl.BlockSpec((n, 256), lambda b, i: (b, 0)))
        args += [kt, v]
    in_specs += [_const_spec((2, A_HD)), _const_spec((2, A_HD)), _const_spec((1, 128))]
    args += [lam_q, lam_k, sg]
    return pl.pallas_call(
        body,
        out_shape=jax.ShapeDtypeStruct((batch * lq, 256), BF16),
        grid=(batch, nq),
        in_specs=in_specs,
        out_specs=pl.BlockSpec((tq, 256), lambda b, i: (b * nq + i, 0)),
        scratch_shapes=[scratch],
        compiler_params=pltpu.CompilerParams(dimension_semantics=("arbitrary", "arbitrary"),
                                             vmem_limit_bytes=V7X_VMEM_LIMIT),
        name="diff_attn_bounded" if bounded else "diff_attn",
    )(*args)


def _local_kernel(pu_ref, gl_ref, dww_ref, dwb_ref, lng_ref, lnb_ref, pww_ref, plw_ref, pls_ref,
                  p_out, c_out, ppad, gpad, psh, gsh, *, seq_len):
    T = LOCAL_TILE
    t = pl.program_id(1)

    @pl.when(t == 0)
    def _():
        z = jnp.zeros((HALO, GROUP_W), F32)
        for pad, src in ((ppad, pu_ref), (gpad, gl_ref)):
            pad[0:HALO, :] = z
            pad[HALO:HALO + seq_len, :] = src[...]
            pad[HALO + seq_len:2 * HALO + seq_len, :] = z

    base = pl.multiple_of(t * T, T)
    win = T + 2 * HALO
    for pad, sh in ((ppad, psh), (gpad, gsh)):
        xx = pad[pl.ds(base, win), :]
        sh[0] = xx[:T + 24]
        for r in range(1, 8):
            sh[r] = pltpu.roll(xx, win - r, 0)[:T + 24]

    def tap(sh, pos):
        a, r = divmod(pos, 8)
        return sh[r, 8 * a:8 * a + T, :]

    lane = lax.broadcasted_iota(jnp.int32, (T, GROUP_W), 1)
    grp = lane >> 6

    acc = tap(psh, HALO - 1) + tap(psh, HALO)
    for o in (-2, 1, -4, -3, 2, 3, -8, -7, -6, -5, 4, 5, 6, 7):
        gmin = 1 if o in (-2, 1) else (2 if o in (-4, -3, 2, 3) else 3)
        acc = acc + jnp.where(grp >= gmin, tap(psh, HALO + o), 0.0)
    tg = t * T + lax.broadcasted_iota(jnp.int32, (T, GROUP_W), 0)
    half = jnp.left_shift(1, grp)
    lo = jnp.maximum(tg - half, 0)
    hi = jnp.minimum(tg + half - 1, seq_len - 1)
    cnt = (hi - lo + 1).astype(F32)
    pooled = acc / cnt - tap(psh, HALO)
    p_out[...] = (_dot(pooled.astype(BF16), plw_ref[...]) * pls_ref[...]).astype(BF16)

    y = tap(gsh, 1) * dww_ref[0:1, :]
    for k in range(1, CONV_K):
        y = y + tap(gsh, k + 1) * dww_ref[k:k + 1, :]
    y = y + dwb_ref[...]
    mu = jnp.mean(y, axis=-1, keepdims=True)
    dlt = y - mu
    var = jnp.mean(dlt * dlt, axis=-1, keepdims=True)
    yn = dlt * lax.rsqrt(var + LN_EPS) * lng_ref[...] + lnb_ref[...]
    act = yn * _sigmoid(yn)
    c_out[...] = _dot(act.astype(BF16), pww_ref[...]).astype(BF16)


def _local_call(pu, glu, dww, dwb, lng, lnb, pww, plw, pls, batch):
    seq_len = pu.shape[0] // batch
    T = LOCAL_TILE
    nt = seq_len // T
    seq_spec = pl.BlockSpec((seq_len, 256), lambda b, t: (b, 0))
    out_spec = pl.BlockSpec((T, 256), lambda b, t: (b * nt + t, 0))
    ob = jax.ShapeDtypeStruct((batch * seq_len, 256), BF16)
    return pl.pallas_call(
        functools.partial(_local_kernel, seq_len=seq_len),
        out_shape=[ob, ob],
        grid=(batch, nt),
        in_specs=[seq_spec, seq_spec, _const_spec((CONV_K, 256)), _const_spec((1, 256)), _const_spec((1, 256)),
                  _const_spec((1, 256)), _const_spec((256, 256)), _const_spec((256, 256)), _const_spec((1, 256))],
        out_specs=[out_spec, out_spec],
        scratch_shapes=[
            pltpu.VMEM((seq_len + 2 * HALO, 256), F32),
            pltpu.VMEM((seq_len + 2 * HALO, 256), F32),
            pltpu.VMEM((8, T + 24, 256), F32),
            pltpu.VMEM((8, T + 24, 256), F32),
        ],
        compiler_params=pltpu.CompilerParams(dimension_semantics=("arbitrary", "arbitrary"),
                                             vmem_limit_bytes=V7X_VMEM_LIMIT),
        name="pool_conv",
    )(pu, glu, dww, dwb, lng, lnb, pww, plw, pls)


def _outffn_kernel(h_ref, m_ref, a_ref, p_ref, c_ref, mod_ref, g2_ref, wo_ref, wi_ref, wf_ref, out_ref, *, d_ff):
    y = (_dot(m_ref[...], wo_ref[0:256, :]) + _dot(a_ref[...], wo_ref[256:512, :])
         + _dot(p_ref[...], wo_ref[512:768, :]) + _dot(c_ref[...], wo_ref[768:1024, :]))
    h1 = h_ref[...] + mod_ref[2:3, :] * y
    ms = jnp.mean(h1 * h1, axis=-1, keepdims=True)
    u = (h1 * lax.rsqrt(ms + RMS_EPS) * g2_ref[...] * (1.0 + mod_ref[4:5, :]) + mod_ref[3:4, :]).astype(BF16)
    acc = jnp.zeros(h1.shape, F32)
    for j in range(d_ff // FFN_CHUNK):
        c0 = j * FFN_CHUNK
        g = _dot(u, wi_ref[:, c0:c0 + FFN_CHUNK])
        up = _dot(u, wi_ref[:, d_ff + c0:d_ff + c0 + FFN_CHUNK])
        act = (g * _sigmoid(g) * up).astype(BF16)
        acc = acc + _dot(act, wf_ref[c0:c0 + FFN_CHUNK, :])
    out_ref[...] = h1 + mod_ref[5:6, :] * acc


def _outffn_call(h, ym, ya, yp, yc, mod4, layer, mod_row_fn, g2, wo, wi, wf):
    rows, d = h.shape
    tm = ROW_TILE
    d_ff = wf.shape[0]
    mix = pl.BlockSpec((tm, 256), lambda i: (i, 0))
    return pl.pallas_call(
        functools.partial(_outffn_kernel, d_ff=d_ff),
        out_shape=jax.ShapeDtypeStruct((rows, d), F32),
        grid=(rows // tm,),
        in_specs=[
            pl.BlockSpec((tm, d), lambda i: (i, 0)), mix, mix, mix, mix,
            pl.BlockSpec((None, None, 6, d), lambda i: (layer, mod_row_fn(i), 0, 0)),
            _const_spec((1, d)), _const_spec((d, d)), _const_spec((d, 2 * d_ff)), _const_spec((d_ff, d)),
        ],
        out_specs=pl.BlockSpec((tm, d), lambda i: (i, 0)),
        compiler_params=pltpu.CompilerParams(dimension_semantics=("arbitrary",), vmem_limit_bytes=V7X_VMEM_LIMIT),
        name="out_proj_ffn",
    )(h, ym, ya, yp, yc, mod4, g2, wo, wi, wf)


def _rope_tables(n_tokens):
    n_rows = n_tokens // GRID_W
    rows = jnp.repeat(jnp.arange(n_rows), GRID_W).astype(F32)
    cols = jnp.tile(jnp.arange(GRID_W), n_rows).astype(F32)
    freqs = ROPE_THETA ** (-jnp.arange(N_FREQ, dtype=F32) / N_FREQ)
    ar, ac = rows[:, None] * freqs, cols[:, None] * freqs
    cos32 = jnp.concatenate([jnp.cos(ar), jnp.cos(ar), jnp.cos(ac), jnp.cos(ac)], axis=1)
    sin32 = jnp.concatenate([-jnp.sin(ar), jnp.sin(ar), -jnp.sin(ac), jnp.sin(ac)], axis=1)
    reps = GROUP_W // A_HD
    return jnp.tile(cos32, (1, reps)), jnp.tile(sin32, (1, reps))


def _pad_lanes(a, n):
    return jnp.pad(a, ((0, 0), (0, n - a.shape[1])))


def _prep_w_in(w):
    mq, mk, mv, mo = (w[:, i * 256:(i + 1) * 256] for i in range(4))
    mg = w[:, 1024:1040].reshape(-1, 2, 2, M_HEADS)
    rest = w[:, 1040:]
    gi = _pad_lanes(mg[:, :, 0, :].reshape(-1, 2 * M_HEADS), 128)
    gf = _pad_lanes(mg[:, :, 1, :].reshape(-1, 2 * M_HEADS), 128)
    return jnp.concatenate([mq, mk * (M_HD ** -0.5), mv, mo, gi, gf, rest], axis=1).astype(BF16)


def kernel(x, c, ctx, c_ctx, norm1_g, norm2_g, mod_w, mod_b, w_in, mlstm_gate_b, mlstm_norm_g, attn_q_norm_g,
           attn_k_norm_g, lambda_q, lambda_k, attn_subln_g, pool_w, pool_scale, conv_dw_w, conv_dw_b, conv_ln_g,
           conv_ln_b, conv_pw_w, w_out, ffn_w_in, ffn_w_out):
    batch, seq, d = x.shape
    lc = ctx.shape[1]
    depth = w_in.shape[0]
    assert seq % ROW_TILE == 0 and (batch * lc) % ROW_TILE == 0 and lc % M_CHUNK == 0 and seq % M_CHUNK == 0
    tiles_per_seq = seq // ROW_TILE

    h_lat = x.reshape(batch * seq, d)
    h_ctx = ctx.reshape(batch * lc, d)
    cc = jnp.concatenate([c, c_ctx[None, :], jnp.zeros((8 - batch - 1, d), F32)], axis=0)
    mod4 = _mod_call(cc, mod_w, mod_b).reshape(depth, 8, 6, d)
    rope_tabs = _rope_tables(seq)
    lat_row = lambda i: i // tiles_per_seq
    ctx_row = lambda i: batch

    for l in range(depth):
        need_ctx = l < depth - 1
        lam_init = 0.8 - 0.6 * math.exp(-0.3 * l)
        w_r = _prep_w_in(w_in[l])
        g1 = norm1_g[l][None, :]
        g2 = norm2_g[l][None, :]
        gq = jnp.tile(attn_q_norm_g[l] * (A_HD ** -0.5 * math.log2(math.e)), GROUP_W // A_HD)[None, :]
        gk = jnp.tile(attn_k_norm_g[l], GROUP_W // A_HD)[None, :]
        gb = mlstm_gate_b[l]
        gb = jnp.concatenate([_pad_lanes(gb[:, 0, :].reshape(1, -1), 128), _pad_lanes(gb[:, 1, :].reshape(1, -1), 128)], axis=1)
        ng = jnp.tile(mlstm_norm_g[l], M_HEADS)[None, :]
        sg = jnp.tile(attn_subln_g[l], 2)[None, :]
        plw = jax.scipy.linalg.block_diag(*[pool_w[l, g] for g in range(len(POOL_WINDOWS))]).astype(BF16)
        pls = pool_scale[l][None, :]
        local_w = (conv_dw_w[l], conv_dw_b[l][None, :], conv_ln_g[l][None, :], conv_ln_b[l][None, :],
                   conv_pw_w[l].astype(BF16), plw, pls)
        wo = w_out[l].astype(BF16)
        wi = ffn_w_in[l].astype(BF16)
        wf = ffn_w_out[l].astype(BF16)

        lat = _inproj_call(h_lat, mod4, l, lat_row, g1, w_r, gq, gk, rope_tabs)
        cx = _inproj_call(h_ctx, mod4, l, ctx_row, g1, w_r, gq, gk, None)
        m_lat, m_ctx = _mlstm_call(cx[0:5], lat[0:5], gb, ng, batch, need_ctx)
        score_bound = 1.02 * A_HD * jnp.max(jnp.abs(gq)) * jnp.max(jnp.abs(gk))
        a_lat = lax.cond(
            score_bound <= SCORE_EXP2_LIMIT,
            lambda q_, kc_, vc_, kl_, vl_: _attn_call(q_, [(kc_, vc_), (kl_, vl_)], lambda_q[l], lambda_k[l], sg,
                                                     batch, lam_init, bounded=True),
            lambda q_, kc_, vc_, kl_, vl_: _attn_call(q_, [(kc_, vc_), (kl_, vl_)], lambda_q[l], lambda_k[l], sg,
                                                     batch, lam_init, bounded=False),
            lat[5], cx[6], cx[7], lat[6], lat[7])
        p_lat, c_lat = _local_call(lat[8], lat[9], *local_w, batch)
        h_lat = _outffn_call(h_lat, m_lat, a_lat, p_lat, c_lat, mod4, l, lat_row, g2, wo, wi, wf)
        if need_ctx:
            a_ctx = _attn_call(cx[5], [(cx[6], cx[7])], lambda_q[l], lambda_k[l], sg, batch, lam_init)
            p_ctx, c_ctx_o = _local_call(cx[8], cx[9], *local_w, batch)
            h_ctx = _outffn_call(h_ctx, m_ctx, a_ctx, p_ctx, c_ctx_o, mod4, l, ctx_row, g2, wo, wi, wf)
    return h_lat.reshape(batch, seq, d)
```
